```python
import jax, jax.numpy as jnp
from jax import lax
import numpy as np

D_MODEL = 1024
BATCH = 4
SEQ = 8192
DEPTH = 2

CHUNK = 64
PLE_DIM = 256
EPS = 1e-6
N_EVEN = (DEPTH + 1) // 2
N_ODD = DEPTH // 2

MIX_WIDTH = D_MODEL
A_HEADS = 8
A_HEAD_DIM = MIX_WIDTH // 2 // A_HEADS
A_WIDTH = A_HEADS * A_HEAD_DIM
GMLP_BLOCK = 128
B_WIDTH = MIX_WIDTH - A_WIDTH
CONV_W = 3
AB_IN = 2 * A_WIDTH + 3 * B_WIDTH

C_HEAD_DIM = 128
C_HEADS = D_MODEL // C_HEAD_DIM
C_WIDTH = C_HEADS * C_HEAD_DIM

D_FF = ((8 * D_MODEL // 3 + 127) // 128) * 128
N_EXPERTS = 8
TOP_K = 2
E_FF = 7 * D_MODEL // 2

kernel_name = "hybrid_gmlp_shortconv_hgrn2_moe_trunk"


def rms_norm(x, g):
    xf = x.astype(jnp.float32)
    y = xf * lax.rsqrt(jnp.mean(xf * xf, axis=-1, keepdims=True) + EPS)
    return (y * g.astype(jnp.float32)).astype(x.dtype)


def layer_norm(x, g, b):
    xf = x.astype(jnp.float32)
    mu = jnp.mean(xf, axis=-1, keepdims=True)
    xc = xf - mu
    y = xc * lax.rsqrt(jnp.mean(xc * xc, axis=-1, keepdims=True) + EPS)
    return (y * g.astype(jnp.float32) + b.astype(jnp.float32)).astype(x.dtype)


def swiglu(h, w_gu, w_down):
    gate, up = jnp.split(h @ w_gu, 2, axis=-1)
    return (jax.nn.silu(gate) * up) @ w_down


def gmlp_spatial_gate(u, v, ln_g, ln_b, w_s, b_s):
    bsz, s, _ = u.shape
    nb = s // GMLP_BLOCK
    v = v.reshape(bsz, s, A_HEADS, A_HEAD_DIM)
    v = layer_norm(v, ln_g.reshape(A_HEADS, A_HEAD_DIM), ln_b.reshape(A_HEADS, A_HEAD_DIM))
    v = v.reshape(bsz, nb, GMLP_BLOCK, A_HEADS, A_HEAD_DIM)
    chunk_id = jnp.arange(GMLP_BLOCK) // CHUNK
    mask = chunk_id[None, :] <= chunk_id[:, None]
    w = jnp.where(mask[None], w_s, 0)
    f = jnp.einsum('hij,bnjhd->bnihd', w, v) + b_s.T[:, :, None]
    return u * f.reshape(bsz, s, A_WIDTH)


def causal_depthwise_conv(x, w):
    return lax.conv_general_dilated(
        x, w[:, None, :], window_strides=(1,), padding=[(CONV_W - 1, 0)],
        dimension_numbers=('NWC', 'WIO', 'NWC'), feature_group_count=x.shape[-1])


def mixer_ab(h, w_in, a_ln_g, a_ln_b, a_ws, a_bs, b_conv, w_out):
    z = h @ w_in
    u, v, xb, cg, bg = jnp.split(
        z, [A_WIDTH, 2 * A_WIDTH, 2 * A_WIDTH + B_WIDTH, 2 * A_WIDTH + 2 * B_WIDTH], axis=-1)
    ya = gmlp_spatial_gate(jax.nn.gelu(u), jax.nn.gelu(v), a_ln_g, a_ln_b, a_ws, a_bs)
    yb = bg * causal_depthwise_conv(cg * xb, b_conv)
    return jnp.concatenate([ya, yb], axis=-1) @ w_out


def hgrn2(h, w_in, lb, o_norm_g, w_out):
    bsz, s, _ = h.shape
    n = s // CHUNK
    q, f, i, g = jnp.split(h @ w_in, 4, axis=-1)
    forget = lb + (1.0 - lb) * jax.nn.sigmoid(f.astype(jnp.float32))
    log_f = jnp.log(forget)
    k = 1.0 - forget
    q = jax.nn.silu(q.astype(jnp.float32))
    i = i.astype(jnp.float32)

    def to_chunks(t):
        return t.reshape(bsz, n, CHUNK, C_HEADS, C_HEAD_DIM).transpose(1, 0, 3, 2, 4)

    causal = jnp.tril(jnp.ones((CHUNK, CHUNK), dtype=bool))

    def step(state, inp):
        qc, kc, ic, lfc = inp
        b = jnp.cumsum(lfc, axis=2)
        o_inter = jnp.einsum('bhtk,bhkv->bhtv', qc * jnp.exp(b), state)
        diff = b[:, :, :, None, :] - b[:, :, None, :, :]
        decay = jnp.exp(jnp.where(causal[:, :, None], diff, -jnp.inf))
        attn = jnp.einsum('bhtk,bhsk,bhtsk->bhts', qc, kc, decay)
        o = o_inter + jnp.einsum('bhts,bhsv->bhtv', attn, ic)
        b_last = b[:, :, -1:, :]
        k_dec = kc * jnp.exp(b_last - b)
        new_state = jnp.exp(b_last[:, :, 0, :])[..., None] * state + \
            jnp.einsum('bhsk,bhsv->bhkv', k_dec, ic)
        return new_state, o

    state0 = jnp.zeros((bsz, C_HEADS, C_HEAD_DIM, C_HEAD_DIM), jnp.float32)
    _, o = lax.scan(step, state0, (to_chunks(q), to_chunks(k), to_chunks(i), to_chunks(log_f)))
    o = o.transpose(1, 0, 3, 2, 4).reshape(bsz, s, C_HEADS, C_HEAD_DIM).astype(h.dtype)
    o = rms_norm(o, o_norm_g.reshape(C_HEADS, C_HEAD_DIM)).reshape(bsz, s, C_WIDTH)
    return (o * jax.nn.silu(g)) @ w_out


def moe_swiglu(h, w_router, b_router, w_gu, w_down):
    logits = (h @ w_router).astype(jnp.float32) + b_router.astype(jnp.float32)
    top_logit, top_idx = lax.top_k(logits, TOP_K)
    top_w = jax.nn.softmax(top_logit, axis=-1)
    gates = jnp.sum(jax.nn.one_hot(top_idx, N_EXPERTS, dtype=jnp.float32) * top_w[..., None], axis=-2)
    gates = gates.astype(h.dtype)
    out = jnp.zeros_like(h)
    for e in range(N_EXPERTS):
        out = out + gates[..., e:e + 1] * swiglu(h, w_gu[e], w_down[e])
    return out


def setup_inputs(seed: int = 0) -> dict:
    key = jax.random.key(seed)
    ks = iter(jax.random.split(key, 32))
    f32 = jnp.float32

    def nrm(shape, scale):
        return jax.random.normal(next(ks), shape, f32) * scale

    return {
        "x": nrm((BATCH, SEQ, D_MODEL), 1.0),
        "p": nrm((DEPTH, BATCH, SEQ, PLE_DIM), 1.0),
        "norm_g": 1.0 + nrm((DEPTH, 4, D_MODEL), 0.1),
        "w_in_ab": nrm((N_EVEN, D_MODEL, AB_IN), D_MODEL ** -0.5),
        "a_ln_g": 1.0 + nrm((N_EVEN, A_WIDTH), 0.1),
        "a_ln_b": nrm((N_EVEN, A_WIDTH), 0.1),
        "a_ws": nrm((N_EVEN, A_HEADS, GMLP_BLOCK, GMLP_BLOCK), GMLP_BLOCK ** -0.5),
        "a_bs": 1.0 + nrm((N_EVEN, A_HEADS, GMLP_BLOCK), 0.1),
        "b_conv": nrm((N_EVEN, CONV_W, B_WIDTH), CONV_W ** -0.5),
        "w_out_ab": nrm((N_EVEN, MIX_WIDTH, D_MODEL), MIX_WIDTH ** -0.5),
        "ffn_w_gu": nrm((N_EVEN, D_MODEL, 2 * D_FF), D_MODEL ** -0.5),
        "ffn_w_down": nrm((N_EVEN, D_FF, D_MODEL), D_FF ** -0.5),
        "w_in_c": nrm((N_ODD, D_MODEL, 4 * C_WIDTH), D_MODEL ** -0.5),
        "lb_param": nrm((DEPTH, C_WIDTH), 0.5),
        "c_onorm_g": 1.0 + nrm((N_ODD, C_WIDTH), 0.1),
        "w_out_c": nrm((N_ODD, C_WIDTH, D_MODEL), C_WIDTH ** -0.5),
        "w_router": nrm((N_ODD, D_MODEL, N_EXPERTS), D_MODEL ** -0.5),
        "b_router": nrm((N_ODD, N_EXPERTS), 0.01),
        "e_w_gu": nrm((N_ODD, N_EXPERTS, D_MODEL, 2 * E_FF), D_MODEL ** -0.5),
        "e_w_down": nrm((N_ODD, N_EXPERTS, E_FF, D_MODEL), E_FF ** -0.5),
        "ple_w_proj": nrm((DEPTH, PLE_DIM, D_MODEL), PLE_DIM ** -0.5),
        "ple_w_gate": nrm((DEPTH, D_MODEL, D_MODEL), D_MODEL ** -0.5),
    }


def reference(x, p, norm_g, w_in_ab, a_ln_g, a_ln_b, a_ws, a_bs, b_conv, w_out_ab,
              ffn_w_gu, ffn_w_down, w_in_c, lb_param, c_onorm_g, w_out_c,
              w_router, b_router, e_w_gu, e_w_down, ple_w_proj, ple_w_gate):
    sm = jax.nn.softmax(lb_param.astype(jnp.float32), axis=0)
    lower_bounds = jnp.cumsum(sm, axis=0) - sm[0:1]

    for layer in range(DEPTH):
        j = layer // 2
        hn = rms_norm(x, norm_g[layer, 0])
        if layer % 2 == 0:
            mix = mixer_ab(hn, w_in_ab[j], a_ln_g[j], a_ln_b[j], a_ws[j], a_bs[j],
                           b_conv[j], w_out_ab[j])
        else:
            mix = hgrn2(hn, w_in_c[j], lower_bounds[layer], c_onorm_g[j], w_out_c[j])
        x = x + rms_norm(mix, norm_g[layer, 1])

        hn = rms_norm(x, norm_g[layer, 2])
        if layer % 2 == 0:
            ffn = swiglu(hn, ffn_w_gu[j], ffn_w_down[j])
        else:
            ffn = moe_swiglu(hn, w_router[j], b_router[j], e_w_gu[j], e_w_down[j])
        x = x + rms_norm(ffn, norm_g[layer, 3])

        gate = jax.nn.sigmoid(x @ ple_w_gate[layer])
        x = x + gate * (p[layer] @ ple_w_proj[layer])
    return x
```

```python
import functools

import jax
import jax.numpy as jnp
from jax import lax
from jax.experimental import pallas as pl
from jax.experimental.pallas import tpu as pltpu

F32 = jnp.float32
BF16 = jnp.bfloat16
I32 = jnp.int32

EPS = 1e-6
D_MODEL = 1024
CHUNK = 64
A_HEADS = 8
A_WIDTH = 512
B_WIDTH = 512
GMLP_BLOCK = 128
C_HEADS = 8
C_HEAD_DIM = 128
D_FF = 2816
N_EXPERTS = 8
E_FF = 3584
PLE_DIM = 256

LANES = 128
SUBLANES = 8
V7X_VMEM_BYTES = 64 * 1024 * 1024
VMEM_LIMIT = V7X_VMEM_BYTES - 8 * 1024 * 1024
ROW_TILES = D_MODEL // LANES

TM_IN_AB = 512
TM_IN_C = 256
TM_AB = 512
TM_OUT = 512
TM_FFN = 512
TF_FFN = 1408
TR_HGRN = 256
TR_ROUTER = 512
TD_DISPATCH = 256
TM_MOE = 512
TF_MOE = 896
TC_COMBINE = 256

HGRN_SUB = 16
HGRN_MAX_EXP = 80.0


def _params(n_axes):
    return pltpu.CompilerParams(dimension_semantics=("arbitrary",) * n_axes, vmem_limit_bytes=VMEM_LIMIT)


def _rms(x, g):
    return x * lax.rsqrt(jnp.mean(x * x, axis=-1, keepdims=True) + EPS) * g


def _dot(a, b):
    return jnp.dot(a, b, preferred_element_type=F32)


def _dot_nt(a, b):
    return lax.dot_general(a, b, (((1,), (1,)), ((), ())), preferred_element_type=F32)


def _dot_tn(a, b):
    return lax.dot_general(a, b, (((0,), (0,)), ((), ())), preferred_element_type=F32)


def _full(shape):
    nd = len(shape)
    return pl.BlockSpec(shape, lambda *_: (0,) * nd)


def _norm_mm_body(x_ref, g_ref, w_ref, o_ref):
    h = _rms(x_ref[...], g_ref[...])
    o_ref[...] = _dot(h.astype(BF16), w_ref[...]).astype(o_ref.dtype)


def _norm_mm(x, g, w, out_dtype, tm):
    t, d = x.shape
    n = w.shape[1]
    return pl.pallas_call(
        _norm_mm_body,
        grid=(t // tm,),
        in_specs=[pl.BlockSpec((tm, d), lambda i: (i, 0)), _full((1, d)), _full((d, n))],
        out_specs=pl.BlockSpec((tm, n), lambda i: (i, 0)),
        out_shape=jax.ShapeDtypeStruct((t, n), out_dtype),
        compiler_params=_params(1),
        name="norm_mm",
    )(x, g.reshape(1, d), w)


def _mixer_ab_body(z_ref, avg_ref, lng_ref, lnb_ref, ws_ref, bs_ref, cw_ref, o_ref, conv_scr, *, tiles_per_seq):
    i = pl.program_id(0)
    tm = z_ref.shape[0]
    gu = jax.nn.gelu(z_ref[:, 0:A_WIDTH].astype(F32))
    gv = jax.nn.gelu(z_ref[:, A_WIDTH:2 * A_WIDTH].astype(F32))

    avg = avg_ref[...]
    mu = _dot(gv.astype(BF16), avg)
    xc = gv - mu
    var = _dot((xc * xc).astype(BF16), avg)
    vn = xc * lax.rsqrt(var + EPS) * lng_ref[...] + lnb_ref[...]

    low = lax.broadcasted_iota(I32, (GMLP_BLOCK, LANES), 1) < (LANES // 2)
    wr = lax.broadcasted_iota(I32, (GMLP_BLOCK, 2 * GMLP_BLOCK), 0)
    wc = lax.broadcasted_iota(I32, (GMLP_BLOCK, 2 * GMLP_BLOCK), 1)
    wmask = ((wc % GMLP_BLOCK) // CHUNK) <= (wr // CHUNK)
    for p in range(A_HEADS // 2):
        cs = slice(p * LANES, (p + 1) * LANES)
        wp = jnp.where(wmask, ws_ref[p], 0.0).astype(BF16)
        for nb in range(tm // GMLP_BLOCK):
            rs = slice(nb * GMLP_BLOCK, (nb + 1) * GMLP_BLOCK)
            vb = vn[rs, cs]
            rhs = jnp.concatenate([jnp.where(low, vb, 0.0), jnp.where(low, 0.0, vb)], axis=0).astype(BF16)
            f = _dot(wp, rhs) + bs_ref[:, cs]
            o_ref[rs, cs] = (gu[rs, cs] * f).astype(o_ref.dtype)

    xb = z_ref[:, 2 * A_WIDTH:2 * A_WIDTH + B_WIDTH].astype(F32)
    cg = z_ref[:, 2 * A_WIDTH + B_WIDTH:2 * A_WIDTH + 2 * B_WIDTH].astype(F32)
    bg = z_ref[:, 2 * A_WIDTH + 2 * B_WIDTH:2 * A_WIDTH + 3 * B_WIDTH].astype(F32)
    prod = cg * xb

    @pl.when(i % tiles_per_seq == 0)
    def _():
        conv_scr[0:SUBLANES, :] = jnp.zeros((SUBLANES, B_WIDTH), F32)

    @pl.when(i % tiles_per_seq != 0)
    def _():
        conv_scr[0:SUBLANES, :] = conv_scr[tm:tm + SUBLANES, :]

    conv_scr[SUBLANES:SUBLANES + tm, :] = prod
    p1 = conv_scr[SUBLANES - 1:SUBLANES - 1 + tm, :]
    p2 = conv_scr[SUBLANES - 2:SUBLANES - 2 + tm, :]
    conv = cw_ref[0:1, :] * p2 + cw_ref[1:2, :] * p1 + cw_ref[2:3, :] * prod
    o_ref[:, A_WIDTH:A_WIDTH + B_WIDTH] = (bg * conv).astype(o_ref.dtype)


def _mixer_ab(z, a_ln_g, a_ln_b, a_ws, a_bs, b_conv, seq):
    t = z.shape[0]
    tm = TM_AB
    hd = A_WIDTH // A_HEADS
    head = jnp.arange(A_WIDTH) // hd
    avg = jnp.where(head[:, None] == head[None, :], 1.0 / hd, 0.0).astype(BF16)
    ws_cat = a_ws.reshape(A_HEADS // 2, 2, GMLP_BLOCK, GMLP_BLOCK).transpose(0, 2, 1, 3)
    ws_cat = ws_cat.reshape(A_HEADS // 2, GMLP_BLOCK, 2 * GMLP_BLOCK)
    bs_full = jnp.repeat(a_bs.T, hd, axis=1)
    body = functools.partial(_mixer_ab_body, tiles_per_seq=seq // tm)
    return pl.pallas_call(
        body,
        grid=(t // tm,),
        in_specs=[
            pl.BlockSpec((tm, z.shape[1]), lambda i: (i, 0)),
            _full((A_WIDTH, A_WIDTH)),
            _full((1, A_WIDTH)),
            _full((1, A_WIDTH)),
            _full((A_HEADS // 2, GMLP_BLOCK, 2 * GMLP_BLOCK)),
            _full((GMLP_BLOCK, A_WIDTH)),
            _full((3, B_WIDTH)),
        ],
        out_specs=pl.BlockSpec((tm, A_WIDTH + B_WIDTH), lambda i: (i, 0)),
        out_shape=jax.ShapeDtypeStruct((t, A_WIDTH + B_WIDTH), BF16),
        scratch_shapes=[pltpu.VMEM((tm + 2 * SUBLANES, B_WIDTH), F32)],
        compiler_params=_params(1),
        name="mixer_ab",
    )(z, avg, a_ln_g.reshape(1, -1), a_ln_b.reshape(1, -1), ws_cat, bs_full, b_conv)


def _mm_norm_res_body(y_ref, w_ref, g_ref, x_ref, o_ref):
    m = _dot(y_ref[...], w_ref[...])
    o_ref[...] = x_ref[...] + _rms(m, g_ref[...])


def _mm_norm_res(y, w, g, x):
    t, d = x.shape
    tm = TM_OUT
    return pl.pallas_call(
        _mm_norm_res_body,
        grid=(t // tm,),
        in_specs=[pl.BlockSpec((tm, y.shape[1]), lambda i: (i, 0)), _full(w.shape), _full((1, d)),
                  pl.BlockSpec((tm, d), lambda i: (i, 0))],
        out_specs=pl.BlockSpec((tm, d), lambda i: (i, 0)),
        out_shape=jax.ShapeDtypeStruct((t, d), F32),
        compiler_params=_params(1),
        name="mm_norm_res",
    )(y, w, g.reshape(1, d), x)


def _ple(x1, p_ref, wproj_ref, wgate_ref):
    gate = jax.nn.sigmoid(_dot(x1.astype(BF16), wgate_ref[...]))
    proj = _dot(p_ref[...].astype(BF16), wproj_ref[...])
    return x1 + gate * proj


def _ffn_body(x_ref, gin_ref, wg_ref, wu_ref, wd_ref, gout_ref, p_ref, wproj_ref, wgate_ref, o_ref, h_scr, acc_scr):
    f = pl.program_id(1)

    @pl.when(f == 0)
    def _():
        h_scr[...] = _rms(x_ref[...], gin_ref[...]).astype(BF16)
        acc_scr[...] = jnp.zeros(acc_scr.shape, F32)

    h = h_scr[...]
    a = jax.nn.silu(_dot(h, wg_ref[...])) * _dot(h, wu_ref[...])
    acc_scr[...] += _dot(a.astype(BF16), wd_ref[...])

    @pl.when(f == pl.num_programs(1) - 1)
    def _():
        x1 = x_ref[...] + _rms(acc_scr[...], gout_ref[...])
        o_ref[...] = _ple(x1, p_ref, wproj_ref, wgate_ref)


def _ffn(x, g_in, w_gu, w_down, g_out, p, w_proj, w_gate):
    t, d = x.shape
    tm, tf = TM_FFN, TF_FFN
    nf = D_FF // tf
    return pl.pallas_call(
        _ffn_body,
        grid=(t // tm, nf),
        in_specs=[
            pl.BlockSpec((tm, d), lambda i, f: (i, 0)),
            _full((1, d)),
            pl.BlockSpec((d, tf), lambda i, f: (0, f)),
            pl.BlockSpec((d, tf), lambda i, f: (0, nf + f)),
            pl.BlockSpec((tf, d), lambda i, f: (f, 0)),
            _full((1, d)),
            pl.BlockSpec((tm, PLE_DIM), lambda i, f: (i, 0)),
            _full((PLE_DIM, d)),
            _full((d, d)),
        ],
        out_specs=pl.BlockSpec((tm, d), lambda i, f: (i, 0)),
        out_shape=jax.ShapeDtypeStruct((t, d), F32),
        scratch_shapes=[pltpu.VMEM((tm, d), BF16), pltpu.VMEM((tm, d), F32)],
        compiler_params=_params(2),
        name="ffn_dense",
    )(x, g_in.reshape(1, d), w_gu, w_gu, w_down, g_out.reshape(1, d), p, w_proj, w_gate)


def _hgrn_body(z_ref, lb_ref, og_ref, tri_ref, o_ref, st_scr):
    j = pl.program_id(1)
    w = C_HEADS * C_HEAD_DIM

    @pl.when(j == 0)
    def _():
        st_scr[...] = jnp.zeros(st_scr.shape, F32)

    lb = lb_ref[...]
    tri = tri_ref[...]
    rr = lax.broadcasted_iota(I32, (CHUNK, CHUNK), 0)
    cc = lax.broadcasted_iota(I32, (CHUNK, CHUNK), 1)
    mask_b = (rr // (2 * HGRN_SUB)) == (cc // (2 * HGRN_SUB))
    mask_c = ((rr // HGRN_SUB) == (cc // HGRN_SUB)) & (rr >= cc)
    row = lax.broadcasted_iota(I32, (CHUNK, C_HEAD_DIM), 0)
    second_half = row >= 2 * HGRN_SUB
    later_b = (row % (2 * HGRN_SUB)) >= HGRN_SUB

    def chunk(c, carry):
        rs = pl.ds(pl.multiple_of(c * CHUNK, CHUNK), CHUNK)
        forget = lb + (1.0 - lb) * jax.nn.sigmoid(z_ref[rs, w:2 * w])
        lf = jnp.log(forget)
        kk = 1.0 - forget
        qs = jax.nn.silu(z_ref[rs, 0:w])
        lf_hi = lf.astype(BF16)
        lf_lo = (lf - lf_hi.astype(F32)).astype(BF16)
        b = _dot(tri, lf_hi) + _dot(tri, lf_lo)
        for h in range(C_HEADS):
            sl = slice(h * C_HEAD_DIM, (h + 1) * C_HEAD_DIM)
            bh, qh, kh = b[:, sl], qs[:, sl], kk[:, sl]
            vh = z_ref[rs, 2 * w + h * C_HEAD_DIM:2 * w + (h + 1) * C_HEAD_DIM].astype(BF16)
            b15, b31, b47, b_last = bh[15:16], bh[31:32], bh[47:48], bh[63:64]
            st = st_scr[h]
            o = _dot_nt((qh * jnp.exp(bh)).astype(BF16), st.astype(BF16))
            xa = jnp.exp(jnp.where(second_half, bh - b31, b31 - bh))
            att = _dot_nt(jnp.where(second_half, qh * xa, 0.0).astype(BF16),
                          jnp.where(second_half, 0.0, kh * xa).astype(BF16))
            ab = jnp.where(second_half, b47, b15)
            xb = jnp.exp(jnp.where(later_b, bh - ab, ab - bh))
            att_b = _dot_nt(jnp.where(later_b, qh * xb, 0.0).astype(BF16),
                            jnp.where(later_b, 0.0, kh * xb).astype(BF16))
            att = att + jnp.where(mask_b, att_b, 0.0)
            ac = jnp.where(row < HGRN_SUB, 0.0, jnp.where(row < 2 * HGRN_SUB, b15, jnp.where(row < 3 * HGRN_SUB, b31, b47)))
            cd = bh - ac
            att_c = _dot_nt((qh * jnp.exp(cd)).astype(BF16),
                            (kh * jnp.exp(jnp.minimum(-cd, HGRN_MAX_EXP))).astype(BF16))
            att = att + jnp.where(mask_c, att_c, 0.0)
            o = o + _dot(att.astype(BF16), vh)
            kdec = (kh * jnp.exp(b_last - bh)).astype(BF16)
            st_scr[h] = st * jnp.exp(b_last) + _dot_tn(vh, kdec)
            on = o * lax.rsqrt(jnp.mean(o * o, axis=-1, keepdims=True) + EPS) * og_ref[:, sl]
            gate = jax.nn.silu(z_ref[rs, 3 * w + h * C_HEAD_DIM:3 * w + (h + 1) * C_HEAD_DIM])
            o_ref[rs, sl] = (on * gate).astype(o_ref.dtype)
        return carry

    lax.fori_loop(0, z_ref.shape[0] // CHUNK, chunk, 0)


def _hgrn(z, lb, o_norm_g, batch, seq):
    w = C_HEADS * C_HEAD_DIM
    tr = TR_HGRN
    nj = seq // tr
    tri = (jnp.arange(CHUNK)[:, None] >= jnp.arange(CHUNK)[None, :]).astype(BF16)
    return pl.pallas_call(
        _hgrn_body,
        grid=(batch, nj),
        in_specs=[pl.BlockSpec((tr, 4 * w), lambda b, j: (b * nj + j, 0)), _full((1, w)), _full((1, w)),
                  _full((CHUNK, CHUNK))],
        out_specs=pl.BlockSpec((tr, w), lambda b, j: (b * nj + j, 0)),
        out_shape=jax.ShapeDtypeStruct((batch * seq, w), BF16),
        scratch_shapes=[pltpu.VMEM((C_HEADS, C_HEAD_DIM, C_HEAD_DIM), F32)],
        compiler_params=_params(2),
        name="hgrn2",
    )(z, lb.reshape(1, w), o_norm_g.reshape(1, w), tri)


def _router_body(x_ref, g_ref, whi_ref, wlo_ref, br_ref, ltri_ref, idx_ref, wt_ref, cnt_ref, carry_scr):
    i = pl.program_id(0)
    tr = x_ref.shape[0]

    @pl.when(i == 0)
    def _():
        carry_scr[...] = jnp.zeros(carry_scr.shape, F32)

    hn = _rms(x_ref[...], g_ref[...])
    hi = hn.astype(BF16)
    lo = (hn - hi.astype(F32)).astype(BF16)
    logits = _dot(hi, whi_ref[...]) + (_dot(hi, wlo_ref[...]) + _dot(lo, whi_ref[...])) + br_ref[...]
    lane = lax.broadcasted_iota(I32, (tr, LANES), 1)
    neg = jnp.float32(-jnp.inf)
    logits = jnp.where(lane < N_EXPERTS, logits, neg)
    m1 = jnp.max(logits, axis=-1, keepdims=True)
    i1 = jnp.min(jnp.where(logits == m1, lane, LANES), axis=-1, keepdims=True)
    rest = jnp.where(lane == i1, neg, logits)
    m2 = jnp.max(rest, axis=-1, keepdims=True)
    i2 = jnp.min(jnp.where(rest == m2, lane, LANES), axis=-1, keepdims=True)
    e2 = jnp.exp(m2 - m1)
    w1 = 1.0 / (1.0 + e2)
    w2 = e2 * w1
    oh1 = lane == i1
    oh2 = lane == i2
    oh = jnp.where(oh1 | oh2, 1.0, 0.0)
    prefix = _dot(ltri_ref[...], oh.astype(BF16)) + carry_scr[...]
    r1 = jnp.sum(jnp.where(oh1, prefix, 0.0), axis=-1, keepdims=True).astype(I32)
    r2 = jnp.sum(jnp.where(oh2, prefix, 0.0), axis=-1, keepdims=True).astype(I32)
    carry_scr[...] += jnp.sum(oh, axis=0, keepdims=True)
    idx_ref[...] = jnp.where(lane == 0, i1, jnp.where(lane == 1, i2, jnp.where(lane == 2, r1, jnp.where(lane == 3, r2, 0))))
    wt_ref[...] = jnp.where(lane == 0, w1, jnp.where(lane == 1, w2, 0.0))
    cnt_ref[...] = carry_scr[...]


def _router(x, g, w_router, b_router):
    t, d = x.shape
    tr = TR_ROUTER
    wpad = jnp.zeros((d, LANES), F32).at[:, :N_EXPERTS].set(w_router)
    whi = wpad.astype(BF16)
    wlo = (wpad - whi.astype(F32)).astype(BF16)
    bpad = jnp.zeros((1, LANES), F32).at[0, :N_EXPERTS].set(b_router)
    ltri = (jnp.arange(tr)[:, None] > jnp.arange(tr)[None, :]).astype(BF16)
    return pl.pallas_call(
        _router_body,
        grid=(t // tr,),
        in_specs=[pl.BlockSpec((tr, d), lambda i: (i, 0)), _full((1, d)), _full((d, LANES)), _full((d, LANES)),
                  _full((1, LANES)), _full((tr, tr))],
        out_specs=[pl.BlockSpec((tr, LANES), lambda i: (i, 0)), pl.BlockSpec((tr, LANES), lambda i: (i, 0)),
                   _full((1, LANES))],
        out_shape=[jax.ShapeDtypeStruct((t, LANES), I32), jax.ShapeDtypeStruct((t, LANES), F32),
                   jax.ShapeDtypeStruct((1, LANES), F32)],
        scratch_shapes=[pltpu.VMEM((1, LANES), F32)],
        compiler_params=_params(1),
        name="router_top2",
    )(x, g.reshape(1, d), whi, wlo, bpad, ltri)


def _row_copy(src, src_row, dst, dst_row, sem):
    s = pl.multiple_of(src_row * ROW_TILES, ROW_TILES)
    d = pl.multiple_of(dst_row * ROW_TILES, ROW_TILES)
    return pltpu.make_async_copy(src.at[pl.ds(s, ROW_TILES)], dst.at[pl.ds(d, ROW_TILES)], sem)


def _dispatch_body(dest_ref, pad_ref, x_ref, g_ref, xs_ref, hn_scr, zero_scr, sem):
    i = pl.program_id(0)
    td = x_ref.shape[0]
    hn = _rms(x_ref[...], g_ref[...])
    for c in range(ROW_TILES):
        hn_scr[pl.ds(c, td, stride=ROW_TILES), :] = hn[:, c * LANES:(c + 1) * LANES]

    def issue(r, carry):
        _row_copy(hn_scr, r, xs_ref, dest_ref[0, 0, 2 * r], sem).start()
        _row_copy(hn_scr, r, xs_ref, dest_ref[0, 0, 2 * r + 1], sem).start()
        return carry

    lax.fori_loop(0, td, issue, 0)

    def drain(r, carry):
        _row_copy(hn_scr, 0, xs_ref, 0, sem).wait()
        return carry

    lax.fori_loop(0, 2 * td, drain, 0)

    @pl.when(i == pl.num_programs(0) - 1)
    def _():
        zero_scr[...] = jnp.zeros(zero_scr.shape, F32)
        n_pad = pad_ref.shape[0]

        def issue_pad(r, carry):
            _row_copy(zero_scr, 0, xs_ref, pad_ref[r], sem).start()
            return carry

        lax.fori_loop(0, n_pad, issue_pad, 0)

        def drain_pad(r, carry):
            _row_copy(zero_scr, 0, xs_ref, 0, sem).wait()
            return carry

        lax.fori_loop(0, n_pad, drain_pad, 0)


def _dispatch(x, g, dest, pad_dest, n_rows):
    t, d = x.shape
    td = TD_DISPATCH
    dest_blk = dest.reshape(t // td, 1, 2 * td)
    return pl.pallas_call(
        _dispatch_body,
        grid=(t // td,),
        in_specs=[
            pl.BlockSpec((1, 1, 2 * td), lambda i: (i, 0, 0), memory_space=pltpu.SMEM),
            pl.BlockSpec(memory_space=pltpu.SMEM),
            pl.BlockSpec((td, d), lambda i: (i, 0)),
            _full((1, d)),
        ],
        out_specs=pl.BlockSpec(memory_space=pl.ANY),
        out_shape=jax.ShapeDtypeStruct((n_rows * ROW_TILES, LANES), F32),
        scratch_shapes=[pltpu.VMEM((td * ROW_TILES, LANES), F32), pltpu.VMEM((ROW_TILES, LANES), F32),
                        pltpu.SemaphoreType.DMA(())],
        compiler_params=_params(1),
        name="moe_dispatch",
    )(dest_blk, pad_dest, x, g.reshape(1, d))


def _moe_body(te_ref, ts_ref, nv_ref, x_ref, wg_ref, wu_ref, wd_ref, o_ref, xb_scr, acc_scr):
    i = pl.program_id(0)
    f = pl.program_id(1)
    tm = xb_scr.shape[0]

    @pl.when(i < nv_ref[0])
    def _():
        @pl.when(f == 0)
        def _():
            for c in range(ROW_TILES):
                xb_scr[:, c * LANES:(c + 1) * LANES] = x_ref[pl.ds(c, tm, stride=ROW_TILES), :].astype(BF16)
            acc_scr[...] = jnp.zeros(acc_scr.shape, F32)

        h = xb_scr[...]
        a = jax.nn.silu(_dot(h, wg_ref[...])) * _dot(h, wu_ref[...])
        acc_scr[...] += _dot(a.astype(BF16), wd_ref[...])

        @pl.when(f == pl.num_programs(1) - 1)
        def _():
            for c in range(ROW_TILES):
                o_ref[pl.ds(c, tm, stride=ROW_TILES), :] = acc_scr[:, c * LANES:(c + 1) * LANES]

    @pl.when((i >= nv_ref[0]) & (f == pl.num_programs(1) - 1))
    def _():
        o_ref[...] = jnp.zeros(o_ref.shape, F32)


def _moe(xs, tile_expert, tile_src, n_valid, w_gu, w_down, n_rows):
    tm, tf = TM_MOE, TF_MOE
    d = D_MODEL
    nf = E_FF // tf
    n_tiles = n_rows // tm

    def f_eff(i, f, nv):
        return jnp.where(i < nv[0], f, nf - 1)

    grid_spec = pltpu.PrefetchScalarGridSpec(
        num_scalar_prefetch=3,
        grid=(n_tiles, nf),
        in_specs=[
            pl.BlockSpec((tm * ROW_TILES, LANES), lambda i, f, te, ts, nv: (ts[i], 0)),
            pl.BlockSpec((None, d, tf), lambda i, f, te, ts, nv: (te[i], 0, f_eff(i, f, nv))),
            pl.BlockSpec((None, d, tf), lambda i, f, te, ts, nv: (te[i], 0, nf + f_eff(i, f, nv))),
            pl.BlockSpec((None, tf, d), lambda i, f, te, ts, nv: (te[i], f_eff(i, f, nv), 0)),
        ],
        out_specs=pl.BlockSpec((tm * ROW_TILES, LANES), lambda i, f, te, ts, nv: (i, 0)),
        scratch_shapes=[pltpu.VMEM((tm, d), BF16), pltpu.VMEM((tm, d), F32)],
    )
    return pl.pallas_call(
        _moe_body,
        grid_spec=grid_spec,
        out_shape=jax.ShapeDtypeStruct((n_rows * ROW_TILES, LANES), F32),
        compiler_params=_params(2),
        name="moe_grouped",
    )(tile_expert, tile_src, n_valid, xs, w_gu, w_gu, w_down)


def _combine_body(dest_ref, ys_ref, wt_ref, x_ref, g_ref, p_ref, wproj_ref, wgate_ref, o_ref, rows_scr, ffn_scr, sem):
    tc = x_ref.shape[0]

    def issue(r, carry):
        _row_copy(ys_ref, dest_ref[0, 0, 2 * r], rows_scr.at[0], r, sem).start()
        _row_copy(ys_ref, dest_ref[0, 0, 2 * r + 1], rows_scr.at[1], r, sem).start()
        return carry

    lax.fori_loop(0, tc, issue, 0)

    def drain(r, carry):
        _row_copy(ys_ref, 0, rows_scr.at[0], 0, sem).wait()
        return carry

    lax.fori_loop(0, 2 * tc, drain, 0)

    w1 = wt_ref[:, 0:1]
    w2 = wt_ref[:, 1:2]
    for c in range(ROW_TILES):
        ffn_scr[:, c * LANES:(c + 1) * LANES] = (w1 * rows_scr[0, pl.ds(c, tc, stride=ROW_TILES), :]
                                                 + w2 * rows_scr[1, pl.ds(c, tc, stride=ROW_TILES), :])
    x1 = x_ref[...] + _rms(ffn_scr[...], g_ref[...])
    o_ref[...] = _ple(x1, p_ref, wproj_ref, wgate_ref)


def _combine(ys, dest, wts, x, g, p, w_proj, w_gate):
    t, d = x.shape
    tc = TC_COMBINE
    dest_blk = dest.reshape(t // tc, 1, 2 * tc)
    return pl.pallas_call(
        _combine_body,
        grid=(t // tc,),
        in_specs=[
            pl.BlockSpec((1, 1, 2 * tc), lambda i: (i, 0, 0), memory_space=pltpu.SMEM),
            pl.BlockSpec(memory_space=pl.ANY),
            pl.BlockSpec((tc, LANES), lambda i: (i, 0)),
            pl.BlockSpec((tc, d), lambda i: (i, 0)),
            _full((1, d)),
            pl.BlockSpec((tc, PLE_DIM), lambda i: (i, 0)),
            _full((PLE_DIM, d)),
            _full((d, d)),
        ],
        out_specs=pl.BlockSpec((tc, d), lambda i: (i, 0)),
        out_shape=jax.ShapeDtypeStruct((t, d), F32),
        scratch_shapes=[pltpu.VMEM((2, tc * ROW_TILES, LANES), F32), pltpu.VMEM((tc, d), F32),
                        pltpu.SemaphoreType.DMA(())],
        compiler_params=_params(1),
        name="moe_combine",
    )(dest_blk, ys, wts, x, g.reshape(1, d), p, w_proj, w_gate)


def _routing_tables(idx_out, cnt_out, n_rows):
    tm = TM_MOE
    n_tiles = n_rows // tm
    counts = cnt_out[0, :N_EXPERTS].astype(I32)
    padded = ((counts + tm - 1) // tm) * tm
    ends = jnp.cumsum(padded)
    offs = ends - padded
    experts = idx_out[:, 0:2]
    ranks = idx_out[:, 2:4]
    dest = offs[experts] + ranks
    n_valid = ends[-1] // tm
    tile_ids = jnp.arange(n_tiles, dtype=I32)
    tile_src = jnp.minimum(tile_ids, n_valid - 1)
    tile_expert = jnp.minimum(jnp.searchsorted(ends // tm, tile_src, side="right"), N_EXPERTS - 1).astype(I32)
    seg_len = jnp.concatenate([padded - counts, (n_rows - ends[-1])[None]])
    seg_start = jnp.concatenate([offs + counts, ends[-1:]])
    seg_end = jnp.cumsum(seg_len)
    j = jnp.arange(n_rows - 2 * idx_out.shape[0], dtype=I32)
    seg = jnp.searchsorted(seg_end, j, side="right")
    pad_dest = seg_start[seg] + (j - (seg_end - seg_len)[seg])
    return dest.astype(I32), tile_expert, tile_src.astype(I32), n_valid.reshape(1).astype(I32), pad_dest.astype(I32)


def _forward(x, p, norm_g, w_in_ab, a_ln_g, a_ln_b, a_ws, a_bs, b_conv, w_out_ab, ffn_w_gu, ffn_w_down, w_in_c,
             lb_param, c_onorm_g, w_out_c, w_router, b_router, e_w_gu, e_w_down, ple_w_proj, ple_w_gate):
    batch, seq, d = x.shape
    t = batch * seq
    xf = x.reshape(t, d)
    pf = p.reshape(p.shape[0], t, PLE_DIM)
    bf = lambda a: a.astype(BF16)

    sm = jax.nn.softmax(lb_param.astype(F32), axis=0)
    lower_bounds = jnp.cumsum(sm, axis=0) - sm[0:1]

    z = _norm_mm(xf, norm_g[0, 0], bf(w_in_ab[0]), BF16, TM_IN_AB)
    y = _mixer_ab(z, a_ln_g[0], a_ln_b[0], a_ws[0], a_bs[0], b_conv[0], seq)
    xf = _mm_norm_res(y, bf(w_out_ab[0]), norm_g[0, 1], xf)
    xf = _ffn(xf, norm_g[0, 2], bf(ffn_w_gu[0]), bf(ffn_w_down[0]), norm_g[0, 3], pf[0], bf(ple_w_proj[0]),
              bf(ple_w_gate[0]))

    z = _norm_mm(xf, norm_g[1, 0], bf(w_in_c[0]), F32, TM_IN_C)
    y = _hgrn(z, lower_bounds[1], c_onorm_g[0], batch, seq)
    xf = _mm_norm_res(y, bf(w_out_c[0]), norm_g[1, 1], xf)

    n_rows = 2 * t + N_EXPERTS * TM_MOE
    idx_out, wts, cnt_out = _router(xf, norm_g[1, 2], w_router[0], b_router[0])
    dest, tile_expert, tile_src, n_valid, pad_dest = _routing_tables(idx_out, cnt_out, n_rows)
    xs = _dispatch(xf, norm_g[1, 2], dest, pad_dest, n_rows)
    ys = _moe(xs, tile_expert, tile_src, n_valid, bf(e_w_gu[0]), bf(e_w_down[0]), n_rows)
    xf = _combine(ys, dest, wts, xf, norm_g[1, 3], pf[1], bf(ple_w_proj[1]), bf(ple_w_gate[1]))
    return xf.reshape(batch, seq, d)


def kernel(x, p, norm_g, w_in_ab, a_ln_g, a_ln_b, a_ws, a_bs, b_conv, w_out_ab, ffn_w_gu, ffn_w_down, w_in_c, lb_param,
           c_onorm_g, w_out_c, w_router, b_router, e_w_gu, e_w_down, ple_w_proj, ple_w_gate):
    return _forward(x, p, norm_g, w_in_ab, a_ln_g, a_ln_b, a_ws, a_bs, b_conv, w_out_ab, ffn_w_gu, ffn_w_down, w_in_c,
                    lb_param, c_onorm_g, w_out_c, w_router, b_router, e_w_gu, e_w_down, ple_w_proj, ple_w_gate)
```
